```python
import jax
import jax.numpy as jnp
from jax import lax
import numpy as np

D_MODEL = 1024
BATCH = 4
SEQ = 8192
DEPTH = 2
DEC_BATCH = 32
DEC_SEQ = 1
PAST_LEN = 16384
PAGE_SIZE = 128

HEAD_DIM = 64
DILATED_GROUPS = ((128, 1), (512, 4), (2048, 16))
N_DIL = len(DILATED_GROUPS)
HEADS_PER_GROUP = 4
ATT_QKV = N_DIL * HEADS_PER_GROUP * HEAD_DIM
ATT_OUT = HEADS_PER_GROUP * HEAD_DIM
CONV_DIM = D_MODEL - ATT_OUT
CONV_WIDTH = 31
D_IN = 3 * ATT_QKV + 2 * CONV_DIM
D_FF = 4 * D_MODEL
ROT_DIM = HEAD_DIM // 4
ROPE_THETA = 500000.0
BLK = 128
ATT_SCALE = HEAD_DIM ** -0.5
RMS_EPS = 1e-6
LN_EPS = 1e-5
NEG_INF = -1e30

kernel_name = "hybrid_dilated_attn_conformer_decoder_step"


def rms_norm(x, g):
    xf = x.astype(jnp.float32)
    y = xf * lax.rsqrt(jnp.mean(xf * xf, axis=-1, keepdims=True) + RMS_EPS)
    return (y * g.astype(jnp.float32)).astype(x.dtype)


def partial_rotary(x, pos):
    half = ROT_DIM // 2
    inv = 1.0 / (ROPE_THETA ** (jnp.arange(0, ROT_DIM, 2, dtype=jnp.float32) / ROT_DIM))
    ang = pos.astype(jnp.float32)[:, None] * inv[None, :]
    bshape = (ang.shape[0],) + (1,) * (x.ndim - 3) + (half,)
    cos = jnp.cos(ang).reshape(bshape)
    sin = jnp.sin(ang).reshape(bshape)
    xr = x[..., :ROT_DIM].astype(jnp.float32)
    x1, x2 = xr[..., :half], xr[..., half:]
    rot = jnp.concatenate([x1 * cos - x2 * sin, x2 * cos + x1 * sin], axis=-1)
    return jnp.concatenate([rot.astype(x.dtype), x[..., ROT_DIM:]], axis=-1)


def dilated_attn_prompt(q, k, v, dil, steps):
    b, s, h, dh = q.shape
    span = dil * BLK
    s_pad = -(-s // span) * span
    length = s_pad // dil
    nb = length // BLK

    def to_sub(t):
        t = jnp.pad(t, ((0, 0), (0, s_pad - s), (0, 0), (0, 0)))
        t = t.reshape(b, length, dil, h, dh).transpose(0, 2, 1, 3, 4)
        return t.reshape(b, dil, nb, BLK, h, dh)

    def with_prev(t):
        prev = jnp.pad(t[:, :, :-1], ((0, 0), (0, 0), (1, 0), (0, 0), (0, 0), (0, 0)))
        return jnp.concatenate([prev, t], axis=3)

    qs = to_sub(q)
    ks = with_prev(to_sub(k))
    vs = with_prev(to_sub(v))
    sc = jnp.einsum('brnqhd,brnkhd->brnhqk', qs, ks, preferred_element_type=jnp.float32) * ATT_SCALE
    qi = jnp.arange(BLK)[:, None]
    kj = jnp.arange(2 * BLK)[None, :] - BLK
    dist = qi - kj
    band = (dist >= 0) & (dist <= steps)
    has_prev = (jnp.arange(nb) > 0)[:, None, None] | (kj >= 0)[None]
    mask = band[None] & has_prev
    sc = jnp.where(mask[None, None, :, None], sc, NEG_INF)
    m = jnp.max(sc, axis=-1, keepdims=True)
    p = jnp.exp(sc - m)
    l = jnp.sum(p, axis=-1, keepdims=True)
    o = jnp.einsum('brnhqk,brnkhd->brnqhd', p, vs.astype(jnp.float32)) / jnp.moveaxis(l, 3, 4)
    lse = jnp.moveaxis((m + jnp.log(l))[..., 0], 3, 4)

    def from_sub(t):
        t = t.reshape(b, dil, length, *t.shape[4:])
        return jnp.moveaxis(t, 1, 2).reshape(b, s_pad, *t.shape[3:])[:, :s]

    return from_sub(o), from_sub(lse)


def dilated_attn_sample(q, kc, vc, n_buf, dil, steps):
    t = q.shape[1]
    idx = n_buf + jnp.arange(t)[:, None] - dil * jnp.arange(steps + 1)[None, :]
    valid = idx >= 0
    idx = jnp.maximum(idx, 0)
    kg = kc[:, idx]
    vg = vc[:, idx]
    sc = jnp.einsum('bthd,btjhd->bhtj', q, kg, preferred_element_type=jnp.float32) * ATT_SCALE
    sc = jnp.where(valid[None, None], sc, NEG_INF)
    m = jnp.max(sc, axis=-1, keepdims=True)
    p = jnp.exp(sc - m)
    l = jnp.sum(p, axis=-1, keepdims=True)
    o = jnp.einsum('bhtj,btjhd->bthd', p, vg.astype(jnp.float32)) / jnp.moveaxis(l, 1, 2)
    lse = jnp.moveaxis((m + jnp.log(l))[..., 0], 1, 2)
    return o, lse


def hybrid_layer(x, pos, kv_bufs, conv_buf, w_in, w_o, conv_w, conv_b, ln_g, ln_b, g_mix, g_ffn, w_up, w_down):
    b, s, _ = x.shape
    n = rms_norm(x, g_mix)
    z = jnp.einsum('bsd,de->bse', n, w_in)
    q, k, v, glu = jnp.split(z, [ATT_QKV, 2 * ATT_QKV, 3 * ATT_QKV], axis=-1)
    shp = (b, s, N_DIL, HEADS_PER_GROUP, HEAD_DIM)
    q = partial_rotary(q.reshape(shp), pos)
    k = partial_rotary(k.reshape(shp), pos)
    v = v.reshape(shp)

    outs, lses, new_kv = [], [], []
    for g, (win, dil) in enumerate(DILATED_GROUPS):
        qg, kg, vg = q[:, :, g], k[:, :, g], v[:, :, g]
        steps = win // dil
        if kv_bufs is None:
            o, lse = dilated_attn_prompt(qg, kg, vg, dil, steps)
            keep = min(win, s)
            new_kv.append(jnp.stack([kg[:, s - keep:], vg[:, s - keep:]], axis=1))
        else:
            buf = kv_bufs[g]
            n_buf = buf.shape[2]
            kc = jnp.concatenate([buf[:, 0].astype(kg.dtype), kg], axis=1)
            vc = jnp.concatenate([buf[:, 1].astype(vg.dtype), vg], axis=1)
            o, lse = dilated_attn_sample(qg, kc, vc, n_buf, dil, steps)
            new_kv.append(jnp.stack([kc[:, -n_buf:], vc[:, -n_buf:]], axis=1))
        outs.append(o)
        lses.append(lse)
    wts = jax.nn.softmax(jnp.stack(lses, axis=0), axis=0)
    att = jnp.sum(wts[..., None] * jnp.stack(outs, axis=0), axis=0)
    att = att.reshape(b, s, ATT_OUT).astype(x.dtype)

    a, gate = jnp.split(glu, 2, axis=-1)
    u = a * jax.nn.sigmoid(gate)
    if conv_buf is None:
        uc = jnp.pad(u, ((0, 0), (CONV_WIDTH - 1, 0), (0, 0)))
    else:
        uc = jnp.concatenate([conv_buf.astype(u.dtype), u], axis=1)
    new_conv = uc[:, -(CONV_WIDTH - 1):]
    c = lax.conv_general_dilated(uc, conv_w[:, None, :].astype(uc.dtype), window_strides=(1,), padding='VALID',
                                 dimension_numbers=('NWC', 'WIO', 'NWC'), feature_group_count=CONV_DIM)
    cf = c.astype(jnp.float32) + conv_b.astype(jnp.float32)
    mu = jnp.mean(cf, axis=-1, keepdims=True)
    var = jnp.mean(jnp.square(cf - mu), axis=-1, keepdims=True)
    cf = (cf - mu) * lax.rsqrt(var + LN_EPS) * ln_g.astype(jnp.float32) + ln_b.astype(jnp.float32)
    cv = (cf * jax.nn.sigmoid(cf)).astype(x.dtype)

    h = x + jnp.einsum('bse,ed->bsd', jnp.concatenate([att, cv], axis=-1), w_o)
    n2 = rms_norm(h, g_ffn)
    f = jnp.square(jax.nn.relu(jnp.einsum('bsd,df->bsf', n2, w_up)))
    y = h + jnp.einsum('bsf,fd->bsd', f, w_down)
    return y, new_kv, new_conv


def setup_inputs(seed: int = 0) -> dict:
    key = jax.random.key(seed)
    ks = jax.random.split(key, 20)
    f32 = jnp.float32

    def nrm(k, shape, scale):
        return jax.random.normal(k, shape, f32) * scale

    def kv_shape(w):
        return (DEPTH, DEC_BATCH, 2, min(w, PAST_LEN), HEADS_PER_GROUP, HEAD_DIM)

    return {
        "x_prompt": nrm(ks[0], (BATCH, SEQ, D_MODEL), 1.0),
        "x_sample": nrm(ks[1], (DEC_BATCH, DEC_SEQ, D_MODEL), 1.0),
        "cache_kv_w128": nrm(ks[2], kv_shape(DILATED_GROUPS[0][0]), 1.0),
        "cache_kv_w512": nrm(ks[3], kv_shape(DILATED_GROUPS[1][0]), 1.0),
        "cache_kv_w2048": nrm(ks[4], kv_shape(DILATED_GROUPS[2][0]), 1.0),
        "state_conv": nrm(ks[5], (DEPTH, DEC_BATCH, CONV_WIDTH - 1, CONV_DIM), 0.5),
        "w_in": nrm(ks[6], (DEPTH, D_MODEL, D_IN), D_MODEL ** -0.5),
        "w_o": nrm(ks[7], (DEPTH, D_MODEL, D_MODEL), D_MODEL ** -0.5),
        "conv_w": nrm(ks[8], (DEPTH, CONV_WIDTH, CONV_DIM), CONV_WIDTH ** -0.5),
        "conv_b": nrm(ks[9], (DEPTH, CONV_DIM), 0.02),
        "conv_ln_g": 1.0 + nrm(ks[10], (DEPTH, CONV_DIM), 0.02),
        "conv_ln_b": nrm(ks[11], (DEPTH, CONV_DIM), 0.02),
        "norm_mix": 1.0 + nrm(ks[12], (DEPTH, D_MODEL), 0.02),
        "norm_ffn": 1.0 + nrm(ks[13], (DEPTH, D_MODEL), 0.02),
        "w_up": nrm(ks[14], (DEPTH, D_MODEL, D_FF), D_MODEL ** -0.5),
        "w_down": nrm(ks[15], (DEPTH, D_FF, D_MODEL), D_FF ** -0.5),
        "norm_final": 1.0 + nrm(ks[16], (D_MODEL,), 0.02),
    }


def reference(x_prompt, x_sample, cache_kv_w128, cache_kv_w512, cache_kv_w2048, state_conv,
              w_in, w_o, conv_w, conv_b, conv_ln_g, conv_ln_b, norm_mix, norm_ffn, w_up, w_down, norm_final):
    pos_p = jnp.arange(x_prompt.shape[1], dtype=jnp.int32)
    pos_s = PAST_LEN + jnp.arange(x_sample.shape[1], dtype=jnp.int32)
    caches = (cache_kv_w128, cache_kv_w512, cache_kv_w2048)
    hp, hs = x_prompt, x_sample
    kv_p = [[] for _ in range(N_DIL)]
    kv_s = [[] for _ in range(N_DIL)]
    conv_p, conv_s = [], []
    for l in range(DEPTH):
        params = (w_in[l], w_o[l], conv_w[l], conv_b[l], conv_ln_g[l], conv_ln_b[l],
                  norm_mix[l], norm_ffn[l], w_up[l], w_down[l])
        hp, nkv_p, ncv_p = hybrid_layer(hp, pos_p, None, None, *params)
        hs, nkv_s, ncv_s = hybrid_layer(hs, pos_s, [c[l] for c in caches], state_conv[l], *params)
        for g in range(N_DIL):
            kv_p[g].append(nkv_p[g])
            kv_s[g].append(nkv_s[g])
        conv_p.append(ncv_p)
        conv_s.append(ncv_s)
    y_prompt = rms_norm(hp, norm_final)
    y_sample = rms_norm(hs, norm_final)
    new_kv_w128_prompt = jnp.stack(kv_p[0])
    new_kv_w512_prompt = jnp.stack(kv_p[1])
    new_kv_w2048_prompt = jnp.stack(kv_p[2])
    new_conv_prompt = jnp.stack(conv_p)
    new_kv_w128_sample = jnp.stack(kv_s[0])
    new_kv_w512_sample = jnp.stack(kv_s[1])
    new_kv_w2048_sample = jnp.stack(kv_s[2])
    new_conv_sample = jnp.stack(conv_s)
    return (y_prompt, y_sample, new_kv_w128_prompt, new_kv_w512_prompt, new_kv_w2048_prompt, new_conv_prompt,
            new_kv_w128_sample, new_kv_w512_sample, new_kv_w2048_sample, new_conv_sample)
```

```python
import functools

import jax
import jax.numpy as jnp
from jax import lax
from jax.experimental import pallas as pl
from jax.experimental.pallas import tpu as pltpu

F32 = jnp.float32
BF16 = jnp.bfloat16

D_MODEL = 1024
HEAD_DIM = 64
HEADS = 4
GROUP_W = HEADS * HEAD_DIM
DILATIONS = (1, 4, 16)
WINDOWS = (128, 512, 2048)
N_GROUPS = 3
QKV_W = N_GROUPS * GROUP_W
CONV_DIM = 768
CONV_WIDTH = 31
D_FF = 4 * D_MODEL
ROT_DIM = 16
ROPE_THETA = 500000.0
BLK = 128
ATT_SCALE = HEAD_DIM ** -0.5
RMS_EPS = 1e-6
LN_EPS = 1e-5
NEG_INF = -1e30
PAST_LEN = 16384

LANES = 128
N_SLABS = CONV_DIM // LANES
SPAN = BLK * DILATIONS[-1]
CONV_HALO = 32
VMEM_LIMIT = 52 * 1024 * 1024


def _rotary_tables(pos):
    half = ROT_DIM // 2
    inv = 1.0 / (ROPE_THETA ** (jnp.arange(0, ROT_DIM, 2, dtype=F32) / ROT_DIM))
    ang = pos.astype(F32)[:, None] * inv[None, :]
    cos, sin = jnp.cos(ang), jnp.sin(ang)
    n = pos.shape[0]
    rest = HEAD_DIM - ROT_DIM
    ones, zeros, z8 = jnp.ones((n, rest), F32), jnp.zeros((n, rest), F32), jnp.zeros((n, half), F32)
    cos64 = jnp.concatenate([cos, cos, ones], axis=-1)
    sa64 = jnp.concatenate([-sin, z8, zeros], axis=-1)
    sb64 = jnp.concatenate([z8, sin, zeros], axis=-1)
    tile = lambda t: jnp.concatenate([t, t], axis=-1)
    return tile(cos64), tile(sa64), tile(sb64)


def _rms_norm(x, g):
    ms = jnp.mean(x * x, axis=-1, keepdims=True)
    return (x * lax.rsqrt(ms + RMS_EPS)) * g


def _rotate(z, cos, sa, sb):
    out = []
    for c in range(N_SLABS):
        zc = z[:, c * LANES:(c + 1) * LANES]
        out.append(zc * cos + pltpu.roll(zc, LANES - ROT_DIM // 2, 1) * sa + pltpu.roll(zc, ROT_DIM // 2, 1) * sb)
    return out


def _project(x, g, w_ref, cos, sa, sb):
    n = _rms_norm(x, g).astype(BF16)
    dot = lambda c0: jnp.dot(n, w_ref[:, c0:c0 + QKV_W], preferred_element_type=F32)
    q = [s * ATT_SCALE for s in _rotate(dot(0), cos, sa, sb)]
    k = _rotate(dot(QKV_W), cos, sa, sb)
    vz = dot(2 * QKV_W)
    v = [vz[:, c * LANES:(c + 1) * LANES] for c in range(N_SLABS)]
    u = dot(3 * QKV_W) * jax.nn.sigmoid(dot(3 * QKV_W + CONV_DIM))
    return q, k, v, u


def _in_proj_kernel(x_ref, g_ref, w_ref, cos_ref, sa_ref, sb_ref,
                    q1_ref, k1_ref, v1_ref, q4_ref, k4_ref, v4_ref, q16_ref, k16_ref, v16_ref,
                    u_ref, t128_ref, t512_ref, t2048_ref, slab_ref, *, tm):
    q, k, v, u = _project(x_ref[...], g_ref[...], w_ref, cos_ref[...], sa_ref[...], sb_ref[...])
    u_ref[...] = u
    outs = ((q1_ref, k1_ref, v1_ref), (q4_ref, k4_ref, v4_ref), (q16_ref, k16_ref, v16_ref))
    tails = (t128_ref, t512_ref, t2048_ref)
    for g, d in enumerate(DILATIONS):
        for p, val in enumerate((q, k, v)):
            for c in range(2):
                z = val[2 * g + c]
                lanes = slice(c * LANES, (c + 1) * LANES)
                if p > 0:
                    keep = min(WINDOWS[g], tm)
                    tails[g][p - 1, :, lanes] = z[tm - keep:, :]
                if d == 1:
                    outs[g][p][:, lanes] = z.astype(BF16)
                else:
                    s = (g - 1) * 6 + p * 2 + c
                    slab_ref[s] = z
                    for r in range(d):
                        outs[g][p][r, :, lanes] = slab_ref[s, pl.ds(r, tm // d, stride=d), :].astype(BF16)


def _in_proj(x, g, w_bf16, tables, *, tm):
    b, s, _ = x.shape
    nt = s // tm
    cos, sa, sb = tables
    tok = lambda bi, i: (bi, i, 0)
    sub = lambda bi, i: (bi, 0, i, 0)
    const2 = lambda bi, i: (0, 0)

    def tail_spec(keep):
        if keep <= tm:
            return pl.BlockSpec((None, 2, keep, GROUP_W), lambda bi, i: (bi, 0, 0, 0))
        first = nt - keep // tm
        return pl.BlockSpec((None, 2, tm, GROUP_W), lambda bi, i: (bi, 0, jnp.maximum(i - first, 0), 0))

    out_shape, out_specs = [], []
    for d in DILATIONS:
        for _ in range(3):
            if d == 1:
                out_shape.append(jax.ShapeDtypeStruct((b, s, GROUP_W), BF16))
                out_specs.append(pl.BlockSpec((None, tm, GROUP_W), tok))
            else:
                out_shape.append(jax.ShapeDtypeStruct((b, d, s // d, GROUP_W), BF16))
                out_specs.append(pl.BlockSpec((None, d, tm // d, GROUP_W), sub))
    out_shape.append(jax.ShapeDtypeStruct((b, s, CONV_DIM), F32))
    out_specs.append(pl.BlockSpec((None, tm, CONV_DIM), tok))
    for w in WINDOWS:
        keep = min(w, s)
        out_shape.append(jax.ShapeDtypeStruct((b, 2, keep, GROUP_W), F32))
        out_specs.append(tail_spec(keep))

    return pl.pallas_call(
        functools.partial(_in_proj_kernel, tm=tm),
        grid=(b, nt),
        in_specs=[
            pl.BlockSpec((None, tm, D_MODEL), tok),
            pl.BlockSpec((1, D_MODEL), const2),
            pl.BlockSpec((D_MODEL, w_bf16.shape[1]), const2, pipeline_mode=pl.Buffered(1)),
            pl.BlockSpec((tm, LANES), lambda bi, i: (i, 0)),
            pl.BlockSpec((tm, LANES), lambda bi, i: (i, 0)),
            pl.BlockSpec((tm, LANES), lambda bi, i: (i, 0)),
        ],
        out_specs=out_specs,
        out_shape=out_shape,
        scratch_shapes=[pltpu.VMEM((12, tm, LANES), F32)],
        compiler_params=pltpu.CompilerParams(
            dimension_semantics=("arbitrary", "arbitrary"), vmem_limit_bytes=VMEM_LIMIT),
        name="in_proj",
    )(x, g.reshape(1, D_MODEL), w_bf16, cos, sa, sb)


def _head_masks(rows):
    lane = lax.broadcasted_iota(jnp.int32, (rows, GROUP_W), 1)
    return [(lane >= h * HEAD_DIM) & (lane < (h + 1) * HEAD_DIM) for h in range(HEADS)]


def _attn_unit(q, kk, vv, mask, hm):
    zero = jnp.zeros_like(q)
    qs = jnp.concatenate([jnp.where(hm[h], q, zero) for h in range(HEADS)], axis=0)
    s = lax.dot_general(qs, kk, (((1,), (1,)), ((), ())), preferred_element_type=F32)
    s = jnp.where(mask, s, NEG_INF)
    m = jnp.max(s, axis=-1, keepdims=True)
    p = jnp.exp(s - m)
    l = jnp.sum(p, axis=-1, keepdims=True)
    o = jnp.dot(p.astype(BF16), vv, preferred_element_type=F32) / l
    lse = m + jnp.log(l)
    o_out = o[0:BLK]
    lse_out = jnp.broadcast_to(lse[0:BLK], (BLK, GROUP_W))
    for h in range(1, HEADS):
        o_out = jnp.where(hm[h], o[h * BLK:(h + 1) * BLK], o_out)
        lse_out = jnp.where(hm[h], lse[h * BLK:(h + 1) * BLK], lse_out)
    return o_out, lse_out


def _attn_kernel(q1, k1, v1, kh1, vh1, q4, k4, v4, kh4, vh4, q16, k16, v16, kh16, vh16,
                 att_ref, o_nat, lse_nat):
    i = pl.program_id(1)
    row = lax.broadcasted_iota(jnp.int32, (HEADS * BLK, 2 * BLK), 0) % BLK
    col = lax.broadcasted_iota(jnp.int32, (HEADS * BLK, 2 * BLK), 1)
    band = (col >= row) & (col <= row + BLK)
    first_mask = band & ((col >= BLK) | (i > 0))
    hm = _head_masks(BLK)

    groups = ((q1, k1, v1, kh1, vh1), (q4, k4, v4, kh4, vh4), (q16, k16, v16, kh16, vh16))
    for g, d in enumerate(DILATIONS):
        q_ref, k_ref, v_ref, kh_ref, vh_ref = groups[g]
        nblk = SPAN // d // BLK
        if d == 1:
            rd = lambda ref, r, start, n: ref[pl.ds(start, n), :]
            rdh = lambda ref, r: ref[...]
        else:
            rd = lambda ref, r, start, n: ref[r, pl.ds(start, n), :]
            rdh = lambda ref, r: ref[r]

        def store(r, nb, o, lse, g=g, d=d):
            start = nb * (BLK * d) + r
            for c in range(2):
                lanes = slice(c * LANES, (c + 1) * LANES)
                if d == 1:
                    idx = pl.ds(pl.multiple_of(start, BLK), BLK)
                else:
                    idx = pl.ds(start, BLK, stride=d)
                o_nat[g, c, idx, :] = o[:, lanes]
                lse_nat[g, c, idx, :] = lse[:, lanes]

        def first_unit(r, rd=rd, rdh=rdh, store=store, q_ref=q_ref, k_ref=k_ref, v_ref=v_ref, kh_ref=kh_ref,
                       vh_ref=vh_ref):
            kk = jnp.concatenate([rdh(kh_ref, r), rd(k_ref, r, 0, BLK)], axis=0)
            vv = jnp.concatenate([rdh(vh_ref, r), rd(v_ref, r, 0, BLK)], axis=0)
            o, lse = _attn_unit(rd(q_ref, r, 0, BLK), kk, vv, first_mask, hm)
            store(r, 0, o, lse)

        def later_unit(u, d=d, rd=rd, store=store, q_ref=q_ref, k_ref=k_ref, v_ref=v_ref):
            r = u % d
            nb = 1 + u // d
            start = pl.multiple_of((nb - 1) * BLK, BLK)
            o, lse = _attn_unit(rd(q_ref, r, start + BLK, BLK), rd(k_ref, r, start, 2 * BLK),
                                rd(v_ref, r, start, 2 * BLK), band, hm)
            store(r, nb, o, lse)

        def run(n_units, fn):
            if n_units == 0:
                return
            if n_units == 1:
                fn(0)
                return
            unroll = 2 if n_units % 2 == 0 else (3 if n_units % 3 == 0 else 1)

            def body(it, carry):
                for j in range(unroll):
                    fn(it * unroll + j)
                return carry
            lax.fori_loop(0, n_units // unroll, body, 0)

        run(d, first_unit)
        run(d * (nblk - 1), later_unit)

    rows = 64

    def merge(t, carry):
        base = pl.multiple_of(t * rows, rows)
        for c in range(2):
            lses = [lse_nat[g, c, pl.ds(base, rows), :] for g in range(N_GROUPS)]
            mx = jnp.maximum(jnp.maximum(lses[0], lses[1]), lses[2])
            es = [jnp.exp(x - mx) for x in lses]
            den = es[0] + es[1] + es[2]
            acc = (es[0] / den) * o_nat[0, c, pl.ds(base, rows), :]
            for g in range(1, N_GROUPS):
                acc = acc + (es[g] / den) * o_nat[g, c, pl.ds(base, rows), :]
            att_ref[pl.ds(base, rows), c * LANES:(c + 1) * LANES] = acc.astype(BF16)
        return carry
    lax.fori_loop(0, SPAN // rows, merge, 0)


def _attention(qkv):
    b, s, _ = qkv[0].shape
    ns = s // SPAN
    in_specs, args = [], []
    for g, d in enumerate(DILATIONS):
        q, k, v = qkv[3 * g:3 * g + 3]
        rows = SPAN // d
        per_blk = rows // BLK
        if d == 1:
            cur = pl.BlockSpec((None, rows, GROUP_W), lambda bi, i: (bi, i, 0))
            halo = pl.BlockSpec((None, BLK, GROUP_W),
                                lambda bi, i, per_blk=per_blk: (bi, jnp.maximum(i * per_blk - 1, 0), 0))
        else:
            cur = pl.BlockSpec((None, d, rows, GROUP_W), lambda bi, i: (bi, 0, i, 0))
            halo = pl.BlockSpec((None, d, BLK, GROUP_W),
                                lambda bi, i, per_blk=per_blk: (bi, 0, jnp.maximum(i * per_blk - 1, 0), 0))
        in_specs += [cur, cur, cur, halo, halo]
        args += [q, k, v, k, v]
    return pl.pallas_call(
        _attn_kernel,
        grid=(b, ns),
        in_specs=in_specs,
        out_specs=pl.BlockSpec((None, SPAN, GROUP_W), lambda bi, i: (bi, i, 0)),
        out_shape=jax.ShapeDtypeStruct((b, s, GROUP_W), BF16),
        scratch_shapes=[pltpu.VMEM((N_GROUPS, 2, SPAN, LANES), F32), pltpu.VMEM((N_GROUPS, 2, SPAN, LANES), F32)],
        compiler_params=pltpu.CompilerParams(
            dimension_semantics=("arbitrary", "arbitrary"), vmem_limit_bytes=VMEM_LIMIT),
        name="dilated_attn",
    )(*args)


def _ffn_tail(x, att, cv, wo_ref, gf, wup_ref, wdn_ref, gfin):
    h = x + jnp.dot(jnp.concatenate([att, cv], axis=-1), wo_ref[...], preferred_element_type=F32)
    n2 = _rms_norm(h, gf).astype(BF16)
    acc = None
    chunk = 1024
    for c in range(D_FF // chunk):
        f = jnp.maximum(jnp.dot(n2, wup_ref[:, c * chunk:(c + 1) * chunk], preferred_element_type=F32), 0.0)
        d = jnp.dot((f * f).astype(BF16), wdn_ref[c * chunk:(c + 1) * chunk, :], preferred_element_type=F32)
        acc = d if acc is None else acc + d
    y = acc + h
    if gfin is not None:
        y = _rms_norm(y, gfin)
    return y


def _layer_norm_swish(slabs, g, b):
    s1 = slabs[0].sum(axis=-1, keepdims=True)
    for c in range(1, N_SLABS):
        s1 = s1 + slabs[c].sum(axis=-1, keepdims=True)
    mu = s1 / CONV_DIM
    cen = [sl - mu for sl in slabs]
    s2 = (cen[0] * cen[0]).sum(axis=-1, keepdims=True)
    for c in range(1, N_SLABS):
        s2 = s2 + (cen[c] * cen[c]).sum(axis=-1, keepdims=True)
    rstd = lax.rsqrt(s2 / CONV_DIM + LN_EPS)
    out = []
    for c in range(N_SLABS):
        lanes = slice(c * LANES, (c + 1) * LANES)
        y = cen[c] * rstd * g[:, lanes] + b[:, lanes]
        out.append(y * jax.nn.sigmoid(y))
    return out


def _mix_ffn_kernel(x_ref, u_ref, uh_ref, att_ref, cw_ref, cb_ref, lg_ref, lb_ref, wo_ref, gf_ref, wup_ref, wdn_ref,
                    gfin_ref, y_ref, us_ref, cv_ref, *, tm, final):
    i = pl.program_id(1)
    halo = jnp.where(i > 0, uh_ref[...], 0.0)
    for c in range(N_SLABS):
        lanes = slice(c * LANES, (c + 1) * LANES)
        us_ref[c, 0:CONV_HALO, :] = halo[:, lanes]
        us_ref[c, CONV_HALO:, :] = u_ref[:, lanes]

    rows = 128
    off = CONV_HALO - (CONV_WIDTH - 1)

    def conv_block(t, carry):
        base = pl.multiple_of(t * rows, rows)
        slabs = []
        for c in range(N_SLABS):
            lanes = slice(c * LANES, (c + 1) * LANES)
            acc = us_ref[c, pl.ds(base + off, rows), :] * cw_ref[0:1, lanes]
            for j in range(1, CONV_WIDTH):
                acc = acc + us_ref[c, pl.ds(base + off + j, rows), :] * cw_ref[j:j + 1, lanes]
            slabs.append(acc + cb_ref[:, lanes])
        cv = _layer_norm_swish(slabs, lg_ref[...], lb_ref[...])
        for c in range(N_SLABS):
            cv_ref[pl.ds(base, rows), c * LANES:(c + 1) * LANES] = cv[c].astype(BF16)
        return carry
    lax.fori_loop(0, tm // rows, conv_block, 0)

    gfin = gfin_ref[...] if final else None
    y_ref[...] = _ffn_tail(x_ref[...], att_ref[...], cv_ref[...], wo_ref, gf_ref[...], wup_ref, wdn_ref, gfin)


def _mix_ffn(x, u, att, cw, cb, lg, lb, wo, gf, wup, wdn, gfin, *, tm, final):
    b, s, _ = x.shape
    tok = lambda bi, i: (bi, i, 0)
    const2 = lambda bi, i: (0, 0)
    per = tm // CONV_HALO
    vec = lambda n: pl.BlockSpec((1, n), const2)
    weight = lambda shape: pl.BlockSpec(shape, const2, pipeline_mode=pl.Buffered(1))
    return pl.pallas_call(
        functools.partial(_mix_ffn_kernel, tm=tm, final=final),
        grid=(b, s // tm),
        in_specs=[
            pl.BlockSpec((None, tm, D_MODEL), tok),
            pl.BlockSpec((None, tm, CONV_DIM), tok),
            pl.BlockSpec((None, CONV_HALO, CONV_DIM), lambda bi, i: (bi, jnp.maximum(i * per - 1, 0), 0)),
            pl.BlockSpec((None, tm, GROUP_W), tok),
            pl.BlockSpec((CONV_HALO, CONV_DIM), const2),
            vec(CONV_DIM), vec(CONV_DIM), vec(CONV_DIM),
            weight((D_MODEL, D_MODEL)),
            vec(D_MODEL),
            weight((D_MODEL, D_FF)),
            weight((D_FF, D_MODEL)),
            vec(D_MODEL),
        ],
        out_specs=pl.BlockSpec((None, tm, D_MODEL), tok),
        out_shape=jax.ShapeDtypeStruct((b, s, D_MODEL), F32),
        scratch_shapes=[pltpu.VMEM((N_SLABS, CONV_HALO + tm, LANES), F32), pltpu.VMEM((tm, CONV_DIM), BF16)],
        compiler_params=pltpu.CompilerParams(
            dimension_semantics=("arbitrary", "arbitrary"), vmem_limit_bytes=VMEM_LIMIT),
        name="mix_ffn",
    )(x, u, u, att, cw, cb.reshape(1, -1), lg.reshape(1, -1), lb.reshape(1, -1), wo, gf.reshape(1, -1), wup, wdn,
      gfin.reshape(1, -1))


def _sample_proj_kernel(x_ref, g_ref, w_ref, cos_ref, sa_ref, sb_ref, q_ref, k_ref, v_ref, u_ref):
    q, k, v, u = _project(x_ref[...], g_ref[...], w_ref, cos_ref[...], sa_ref[...], sb_ref[...])
    for c in range(N_SLABS):
        lanes = slice(c * LANES, (c + 1) * LANES)
        q_ref[:, lanes] = q[c]
        k_ref[:, lanes] = k[c]
        v_ref[:, lanes] = v[c]
    u_ref[...] = u


def _sample_proj(x, g, w_bf16, tables):
    n = x.shape[0]
    cos, sa, sb = tables
    sds = jax.ShapeDtypeStruct((n, QKV_W), F32)
    return pl.pallas_call(
        _sample_proj_kernel,
        out_shape=[sds, sds, sds, jax.ShapeDtypeStruct((n, CONV_DIM), F32)],
        compiler_params=pltpu.CompilerParams(vmem_limit_bytes=VMEM_LIMIT),
        name="sample_proj",
    )(x, g.reshape(1, D_MODEL), w_bf16, cos, sa, sb)


def _sample_attn_kernel(q_ref, kn_ref, vn_ref, c1_ref, c4_ref, c16_ref, att_ref):
    os, lses = [], []
    for g, c_ref in enumerate((c1_ref, c4_ref, c16_ref)):
        q = q_ref[:, g]
        kn, vn = kn_ref[:, g], vn_ref[:, g]
        kc, vc = c_ref[:, 0], c_ref[:, 1]
        s0 = jnp.sum(q * kn, axis=-1, keepdims=True)
        s = jnp.sum(q[:, None] * kc, axis=-1, keepdims=True)
        m = jnp.maximum(jnp.max(s, axis=1), s0)
        p0 = jnp.exp(s0 - m)
        p = jnp.exp(s - m[:, None])
        l = p0 + jnp.sum(p, axis=1)
        o = (p0 * vn + jnp.sum(p * vc, axis=1)) / l
        os.append(o)
        lses.append(m + jnp.log(l))
    mx = jnp.maximum(jnp.maximum(lses[0], lses[1]), lses[2])
    es = [jnp.exp(x - mx) for x in lses]
    den = es[0] + es[1] + es[2]
    att_ref[...] = (es[0] / den) * os[0] + (es[1] / den) * os[1] + (es[2] / den) * os[2]


def _sample_attn(q, kn, vn, caches, layer, *, bb=8):
    n = q.shape[0]
    new = pl.BlockSpec((bb, N_GROUPS, HEADS, HEAD_DIM), lambda i: (i, 0, 0, 0))
    in_specs, views = [new, new, new], []
    for cache, d in zip(caches, DILATIONS):
        depth, b, _, n_buf, _, _ = cache.shape
        views.append(cache.reshape(depth, b, 2, n_buf // d, d, HEADS, HEAD_DIM))
        in_specs.append(pl.BlockSpec((None, bb, 2, n_buf // d, None, HEADS, HEAD_DIM),
                                     lambda i, layer=layer: (layer, i, 0, 0, 0, 0, 0)))
    return pl.pallas_call(
        _sample_attn_kernel,
        grid=(n // bb,),
        in_specs=in_specs,
        out_specs=pl.BlockSpec((bb, HEADS, HEAD_DIM), lambda i: (i, 0, 0)),
        out_shape=jax.ShapeDtypeStruct((n, HEADS, HEAD_DIM), F32),
        compiler_params=pltpu.CompilerParams(dimension_semantics=("arbitrary",), vmem_limit_bytes=VMEM_LIMIT),
        name="sample_attn",
    )(q, kn, vn, *views)


def _sample_mix_kernel(x_ref, u_ref, st_ref, att_ref, cw_ref, cb_ref, lg_ref, lb_ref, wo_ref, gf_ref, wup_ref,
                       wdn_ref, gfin_ref, y_ref, *, final):
    hist = CONV_WIDTH - 1
    slabs = []
    for c in range(N_SLABS):
        lanes = slice(c * LANES, (c + 1) * LANES)
        past = jnp.sum(st_ref[:, :, lanes] * cw_ref[0:hist, lanes][None], axis=1)
        slabs.append(past + u_ref[:, lanes] * cw_ref[hist:hist + 1, lanes] + cb_ref[:, lanes])
    cv = _layer_norm_swish(slabs, lg_ref[...], lb_ref[...])
    cv = jnp.concatenate(cv, axis=-1).astype(BF16)
    gfin = gfin_ref[...] if final else None
    y_ref[...] = _ffn_tail(x_ref[...], att_ref[...].astype(BF16), cv, wo_ref, gf_ref[...], wup_ref, wdn_ref, gfin)


def _sample_mix(x, u, state, att, cw, cb, lg, lb, wo, gf, wup, wdn, gfin, *, final):
    n = x.shape[0]
    return pl.pallas_call(
        functools.partial(_sample_mix_kernel, final=final),
        out_shape=jax.ShapeDtypeStruct((n, D_MODEL), F32),
        compiler_params=pltpu.CompilerParams(vmem_limit_bytes=VMEM_LIMIT),
        name="sample_mix",
    )(x, u, state, att, cw, cb.reshape(1, -1), lg.reshape(1, -1), lb.reshape(1, -1), wo, gf.reshape(1, -1), wup, wdn,
      gfin.reshape(1, -1))


def kernel(x_prompt, x_sample, cache_kv_w128, cache_kv_w512, cache_kv_w2048, state_conv, w_in, w_o, conv_w, conv_b,
           conv_ln_g, conv_ln_b, norm_mix, norm_ffn, w_up, w_down, norm_final):
    depth = w_in.shape[0]
    b, s, _ = x_prompt.shape
    nb, t, _ = x_sample.shape
    assert t == 1 and s % SPAN == 0
    tm = 512
    caches = (cache_kv_w128, cache_kv_w512, cache_kv_w2048)
    for cache, w, d in zip(caches, WINDOWS, DILATIONS):
        assert cache.shape[3] == w == BLK * d

    tab_p = _rotary_tables(jnp.arange(s, dtype=jnp.int32))
    tab_s = _rotary_tables(jnp.full((nb,), PAST_LEN, dtype=jnp.int32))
    cw_pad = jnp.pad(conv_w, ((0, 0), (0, CONV_HALO - CONV_WIDTH), (0, 0)))
    w_in_b, w_o_b, w_up_b, w_down_b = (w.astype(BF16) for w in (w_in, w_o, w_up, w_down))

    hp = x_prompt
    hs = x_sample.reshape(nb, D_MODEL)
    kv_p = [[] for _ in range(N_GROUPS)]
    new_rows = [[] for _ in range(N_GROUPS)]
    conv_p, conv_s = [], []
    for l in range(depth):
        final = l == depth - 1
        mix_args = (cw_pad[l], conv_b[l], conv_ln_g[l], conv_ln_b[l], w_o_b[l], norm_ffn[l], w_up_b[l], w_down_b[l],
                    norm_final)
        outs = _in_proj(hp, norm_mix[l], w_in_b[l], tab_p, tm=tm)
        qkv, u, tails = outs[:9], outs[9], outs[10:]
        att = _attention(qkv)
        hp = _mix_ffn(hp, u, att, *mix_args, tm=tm, final=final)
        for g in range(N_GROUPS):
            kv_p[g].append(tails[g].reshape(b, 2, -1, HEADS, HEAD_DIM))
        conv_p.append(u[:, s - (CONV_WIDTH - 1):])
        q_s, k_s, v_s, u_s = _sample_proj(hs, norm_mix[l], w_in_b[l], tab_s)
        split = lambda z: z.reshape(nb, N_GROUPS, HEADS, HEAD_DIM)
        att_s = _sample_attn(split(q_s), split(k_s), split(v_s), caches, l)
        hs = _sample_mix(hs, u_s, state_conv[l], att_s.reshape(nb, GROUP_W), *mix_args, final=final)
        for g in range(N_GROUPS):
            new_rows[g].append(jnp.stack([split(k_s)[:, g], split(v_s)[:, g]], axis=1))
        conv_s.append(jnp.concatenate([state_conv[l][:, 1:], u_s[:, None]], axis=1))

    kv_s = []
    for g in range(N_GROUPS):
        rows = jnp.stack(new_rows[g])[:, :, :, None]
        kv_s.append(jnp.concatenate([caches[g][:, :, :, 1:], rows], axis=3))
    return (hp, hs.reshape(nb, 1, D_MODEL),
            jnp.stack(kv_p[0]), jnp.stack(kv_p[1]), jnp.stack(kv_p[2]), jnp.stack(conv_p),
            kv_s[0], kv_s[1], kv_s[2], jnp.stack(conv_s))
```
